```python
import jax, jax.numpy as jnp
from jax import lax
import numpy as np

D_MODEL = 4096
BATCH = 4
SEQ = 4096
DEPTH = 1

MEM_LEN = 256
MIX_WIDTH = D_MODEL
GROUP_DIM = 128
CONV_WIDTH = MIX_WIDTH // 2
SCONV_WIDTH = MIX_WIDTH - CONV_WIDTH
N_CONV_GROUPS = CONV_WIDTH // GROUP_DIM
N_SCONV_HEADS = SCONV_WIDTH // GROUP_DIM
IN_WIDTH = 2 * CONV_WIDTH + 3 * SCONV_WIDTH
CONV_KERNEL = 31
SHORT_KERNEL = 3
FFN_KERNEL = 3
D_FF = ((8 * D_MODEL // 3 + 255) // 256) * 256
N_XATTN_HEADS = 4
XATTN_HEAD_DIM = D_MODEL // N_XATTN_HEADS
EPS = 1e-6

kernel_name = "hybrid_conformer_shortconv_xattn_convffn"


def rms_norm(x, g):
    xf = x.astype(jnp.float32)
    y = xf * lax.rsqrt(jnp.mean(xf * xf, axis=-1, keepdims=True) + EPS)
    return (y * g.astype(jnp.float32)).astype(x.dtype)


def group_layer_norm(x, g, b, group):
    shape = x.shape
    xf = x.astype(jnp.float32).reshape(*shape[:-1], shape[-1] // group, group)
    mu = jnp.mean(xf, axis=-1, keepdims=True)
    xc = xf - mu
    var = jnp.mean(xc * xc, axis=-1, keepdims=True)
    y = (xc * lax.rsqrt(var + EPS)).reshape(shape)
    return (y * g.astype(jnp.float32) + b.astype(jnp.float32)).astype(x.dtype)


def causal_dwconv(x, w):
    K = w.shape[0]
    T = x.shape[1]
    xp = jnp.pad(x, ((0, 0), (K - 1, 0), (0, 0)))
    y = xp[:, 0:T, :] * w[0]
    for k in range(1, K):
        y = y + xp[:, k:k + T, :] * w[k]
    return y


def setup_inputs(seed: int = 0) -> dict:
    key = jax.random.key(seed)
    ks = jax.random.split(key, 24)

    def nrm(k, shape, scale):
        return jax.random.normal(k, shape, jnp.float32) * scale

    def gain(k, shape):
        return 1.0 + 0.02 * jax.random.normal(k, shape, jnp.float32)

    L = DEPTH
    return {
        "x": nrm(ks[0], (BATCH, SEQ, D_MODEL), 1.0),
        "mem": nrm(ks[1], (BATCH, MEM_LEN, D_MODEL), 1.0),
        "g_mix": gain(ks[2], (L, D_MODEL)),
        "w_in": nrm(ks[3], (L, D_MODEL, IN_WIDTH), D_MODEL ** -0.5),
        "conv_a_w": nrm(ks[4], (L, CONV_KERNEL, CONV_WIDTH), CONV_KERNEL ** -0.5),
        "conv_a_b": nrm(ks[5], (L, CONV_WIDTH), 0.02),
        "ln_a_g": gain(ks[6], (L, CONV_WIDTH)),
        "ln_a_b": nrm(ks[7], (L, CONV_WIDTH), 0.02),
        "conv_b_w": nrm(ks[8], (L, SHORT_KERNEL, SCONV_WIDTH), SHORT_KERNEL ** -0.5),
        "w_out": nrm(ks[9], (L, MIX_WIDTH, D_MODEL), MIX_WIDTH ** -0.5),
        "g_xattn": gain(ks[10], (L, D_MODEL)),
        "g_mem": gain(ks[11], (D_MODEL,)),
        "w_q": nrm(ks[12], (L, D_MODEL, D_MODEL), D_MODEL ** -0.5),
        "w_k": nrm(ks[13], (L, D_MODEL, D_MODEL), D_MODEL ** -0.5),
        "w_v": nrm(ks[14], (L, D_MODEL, D_MODEL), D_MODEL ** -0.5),
        "w_o": nrm(ks[15], (L, D_MODEL, D_MODEL), D_MODEL ** -0.5),
        "g_ffn": gain(ks[16], (L, D_MODEL)),
        "w_gate": nrm(ks[17], (L, D_MODEL, D_FF), D_MODEL ** -0.5),
        "w_up": nrm(ks[18], (L, D_MODEL, D_FF), D_MODEL ** -0.5),
        "conv_f_w": nrm(ks[19], (L, FFN_KERNEL, D_FF), FFN_KERNEL ** -0.5),
        "w_down": nrm(ks[20], (L, D_FF, D_MODEL), D_FF ** -0.5),
        "g_final": gain(ks[21], (D_MODEL,)),
    }


def reference(x, mem, g_mix, w_in, conv_a_w, conv_a_b, ln_a_g, ln_a_b, conv_b_w,
              w_out, g_xattn, g_mem, w_q, w_k, w_v, w_o, g_ffn, w_gate, w_up,
              conv_f_w, w_down, g_final):
    B, T, D = x.shape
    M = mem.shape[1]
    memn = rms_norm(mem, g_mem)
    splits = [CONV_WIDTH, 2 * CONV_WIDTH, 2 * CONV_WIDTH + SCONV_WIDTH,
              2 * CONV_WIDTH + 2 * SCONV_WIDTH]
    h = x
    for l in range(DEPTH):
        xn = rms_norm(h, g_mix[l])
        proj = xn @ w_in[l]
        a_val, a_gate, b_gate, c_gate, b_h = jnp.split(proj, splits, axis=-1)
        u = a_val * jax.nn.sigmoid(a_gate)
        u = causal_dwconv(u, conv_a_w[l]) + conv_a_b[l]
        u = jax.nn.silu(group_layer_norm(u, ln_a_g[l], ln_a_b[l], GROUP_DIM))
        v = b_gate * causal_dwconv(c_gate * b_h, conv_b_w[l])
        mix = jnp.concatenate([u, v], axis=-1)
        h = h + mix @ w_out[l]

        xn = rms_norm(h, g_xattn[l])
        q = (xn @ w_q[l]).reshape(B, T, N_XATTN_HEADS, XATTN_HEAD_DIM)
        k = (memn @ w_k[l]).reshape(B, M, N_XATTN_HEADS, XATTN_HEAD_DIM)
        vm = (memn @ w_v[l]).reshape(B, M, N_XATTN_HEADS, XATTN_HEAD_DIM)
        s = jnp.einsum('bthd,bmhd->bhtm', q.astype(jnp.float32), k.astype(jnp.float32))
        p = jax.nn.softmax(s * (XATTN_HEAD_DIM ** -0.5), axis=-1).astype(vm.dtype)
        o = jnp.einsum('bhtm,bmhd->bthd', p, vm).reshape(B, T, D)
        h = h + o @ w_o[l]

        xn = rms_norm(h, g_ffn[l])
        g = causal_dwconv(xn @ w_gate[l], conv_f_w[l])
        f = jax.nn.silu(g) * (xn @ w_up[l])
        h = h + f @ w_down[l]
    return rms_norm(h, g_final)
```

```python
import functools

import jax
import jax.numpy as jnp
from jax import lax
from jax.experimental import pallas as pl
from jax.experimental.pallas import tpu as pltpu

GROUP_DIM = 128
N_XATTN_HEADS = 4
EPS = 1e-6

V7X_BF16_SUBLANES = 16
V7X_VMEM_BYTES = 64 * 1024 * 1024

TOKEN_TILE = 1024
NORM_ROWS = 256
CONV_ROWS = 64
HALO_A = 2 * V7X_BF16_SUBLANES
HALO_3 = V7X_BF16_SUBLANES

F32 = jnp.float32
BF16 = jnp.bfloat16


def _vmem_limit(nbytes):
    return int(min(nbytes * 5 // 4 + (4 << 20), V7X_VMEM_BYTES - (6 << 20)))


def _params(nbytes, n_axes):
    return pltpu.CompilerParams(dimension_semantics=("arbitrary",) * n_axes,
                                vmem_limit_bytes=_vmem_limit(nbytes))


def _sigmoid(x):
    return 1.0 / (1.0 + jnp.exp(-x))


def _dot(a, b):
    return jnp.dot(a, b, preferred_element_type=F32)


def _rmsnorm_kernel(x_ref, g_ref, o_ref):
    x = x_ref[...]
    r = lax.rsqrt(jnp.mean(x * x, axis=-1, keepdims=True) + EPS)
    o_ref[...] = ((x * r) * g_ref[...]).astype(o_ref.dtype)


def _rmsnorm(x2d, g, out_dtype):
    n, d = x2d.shape
    rows = min(NORM_ROWS, n)
    return pl.pallas_call(
        _rmsnorm_kernel,
        grid=(n // rows,),
        in_specs=[pl.BlockSpec((rows, d), lambda i: (i, 0)),
                  pl.BlockSpec((1, d), lambda i: (0, 0))],
        out_specs=pl.BlockSpec((rows, d), lambda i: (i, 0)),
        out_shape=jax.ShapeDtypeStruct((n, d), out_dtype),
        compiler_params=_params(2 * rows * d * (4 + jnp.dtype(out_dtype).itemsize), 1),
        name="rmsnorm",
    )(x2d, g.reshape(1, d))


def _matmul_kernel(*refs, n_lhs, has_res):
    a_refs = refs[:n_lhs]
    w_refs = refs[n_lhs:2 * n_lhs]
    o_ref = refs[-1]
    acc = _dot(a_refs[0][...], w_refs[0][...])
    for a_ref, w_ref in zip(a_refs[1:], w_refs[1:]):
        acc = acc + _dot(a_ref[...], w_ref[...])
    if has_res:
        acc = acc + refs[2 * n_lhs][...]
    o_ref[...] = acc.astype(o_ref.dtype)


def _matmul(lhs_list, w, res, out_dtype, tm, tn, name):
    n_lhs = len(lhs_list)
    m, kp = lhs_list[0].shape
    k, n = w.shape
    assert kp * n_lhs == k and m % tm == 0 and n % tn == 0
    in_specs = [pl.BlockSpec((tm, kp), lambda i, j: (i, 0)) for _ in lhs_list]
    in_specs += [pl.BlockSpec((kp, tn), functools.partial(lambda i, j, p: (p, j), p=p))
                 for p in range(n_lhs)]
    args = list(lhs_list) + [w] * n_lhs
    out_size = jnp.dtype(out_dtype).itemsize
    nbytes = 2 * (tm * k * 2 + k * tn * 2 + tm * tn * out_size) + tm * tn * 4
    if res is not None:
        in_specs.append(pl.BlockSpec((tm, tn), lambda i, j: (i, j)))
        args.append(res)
        nbytes += 2 * tm * tn * 4
    return pl.pallas_call(
        functools.partial(_matmul_kernel, n_lhs=n_lhs, has_res=res is not None),
        grid=(m // tm, n // tn),
        in_specs=in_specs,
        out_specs=pl.BlockSpec((tm, tn), lambda i, j: (i, j)),
        out_shape=jax.ShapeDtypeStruct((m, n), out_dtype),
        compiler_params=_params(nbytes, 2),
        name=name,
    )(*args)


def _stage_rows(halo_ref, main_ref, xs_ref, halo):
    @pl.when(pl.program_id(1) == 0)
    def _():
        xs_ref[pl.ds(0, halo), :] = halo_ref[...]
        xs_ref[pl.ds(halo, main_ref.shape[0]), :] = main_ref[...]


def _halo_specs(tm, halo, d):
    main = pl.BlockSpec((tm, d), lambda i, j: (i, 0))
    prev = pl.BlockSpec((halo, d), lambda i, j: (jnp.maximum(i * (tm // halo) - 1, 0), 0))
    return prev, main


def _causal_taps(src_ref, w_ref, row0, rows, halo, lanes):
    n_taps = w_ref.shape[0]
    acc = None
    for k in range(n_taps):
        start = halo + row0 - (n_taps - 1) + k
        term = src_ref[pl.ds(start, rows), lanes] * w_ref[pl.ds(k, 1), lanes]
        acc = term if acc is None else acc + term
    return acc


def _mix_a_kernel(halo_ref, main_ref, wv_ref, wg_ref, cw_ref, cb_ref, lg_ref, lb_ref,
                  o_ref, xs_ref, val_ref, gate_ref, u_ref, *, tiles_per_seq):
    tm, tn = o_ref.shape
    _stage_rows(halo_ref, main_ref, xs_ref, HALO_A)
    val_ref[...] = _dot(xs_ref[...], wv_ref[...])
    gate_ref[...] = _dot(xs_ref[...], wg_ref[...])

    seq_start = pl.program_id(0) % tiles_per_seq == 0
    head = pl.ds(0, HALO_A)
    u_head = val_ref[head, :] * _sigmoid(gate_ref[head, :])
    u_ref[head, :] = jnp.where(seq_start, 0.0, u_head)

    for c in range(tm // CONV_ROWS):
        row0 = c * CONV_ROWS
        rows = pl.ds(HALO_A + row0, CONV_ROWS)
        u_ref[rows, :] = val_ref[rows, :] * _sigmoid(gate_ref[rows, :])
        for g in range(tn // GROUP_DIM):
            lanes = pl.ds(g * GROUP_DIM, GROUP_DIM)
            y = _causal_taps(u_ref, cw_ref, row0, CONV_ROWS, HALO_A, lanes) + cb_ref[:, lanes]
            mu = jnp.mean(y, axis=-1, keepdims=True)
            yc = y - mu
            var = jnp.mean(yc * yc, axis=-1, keepdims=True)
            yn = (yc * lax.rsqrt(var + EPS)) * lg_ref[:, lanes] + lb_ref[:, lanes]
            o_ref[pl.ds(row0, CONV_ROWS), lanes] = (yn * _sigmoid(yn)).astype(o_ref.dtype)


def _mix_a(xn, w_in, conv_w, conv_b, ln_g, ln_b, seq_len, tn=256):
    m, d = xn.shape
    width = conv_w.shape[1]
    tm = TOKEN_TILE
    nj = width // tn
    prev, main = _halo_specs(tm, HALO_A, d)
    vec = lambda r: pl.BlockSpec((r, tn), lambda i, j: (0, j))
    nbytes = (3 * (tm + HALO_A) * d * 2 + 4 * d * tn * 2
              + 3 * (tm + HALO_A) * tn * 4 + 2 * tm * tn * 2)
    return pl.pallas_call(
        functools.partial(_mix_a_kernel, tiles_per_seq=seq_len // tm),
        grid=(m // tm, nj),
        in_specs=[prev, main,
                  pl.BlockSpec((d, tn), lambda i, j: (0, j)),
                  pl.BlockSpec((d, tn), lambda i, j: (0, nj + j)),
                  vec(conv_w.shape[0]), vec(1), vec(1), vec(1)],
        out_specs=pl.BlockSpec((tm, tn), lambda i, j: (i, j)),
        out_shape=jax.ShapeDtypeStruct((m, width), BF16),
        scratch_shapes=[pltpu.VMEM((tm + HALO_A, d), BF16),
                        pltpu.VMEM((tm + HALO_A, tn), F32),
                        pltpu.VMEM((tm + HALO_A, tn), F32),
                        pltpu.VMEM((tm + HALO_A, tn), F32)],
        compiler_params=_params(nbytes, 2),
        name="mix_a",
    )(xn, xn, w_in, w_in, conv_w, conv_b.reshape(1, -1), ln_g.reshape(1, -1),
      ln_b.reshape(1, -1))


def _mix_b_kernel(halo_ref, main_ref, wb_ref, wc_ref, wh_ref, cw_ref,
                  o_ref, xs_ref, bg_ref, z_ref, *, tiles_per_seq):
    tm, tn = o_ref.shape
    _stage_rows(halo_ref, main_ref, xs_ref, HALO_3)
    bg_ref[...] = _dot(xs_ref[pl.ds(HALO_3, tm), :], wb_ref[...])
    z_ref[...] = _dot(xs_ref[...], wc_ref[...]) * _dot(xs_ref[...], wh_ref[...])

    @pl.when(pl.program_id(0) % tiles_per_seq == 0)
    def _():
        z_ref[pl.ds(0, HALO_3), :] = jnp.zeros((HALO_3, tn), F32)

    lanes = pl.ds(0, tn)
    for c in range(tm // CONV_ROWS):
        row0 = c * CONV_ROWS
        rows = pl.ds(row0, CONV_ROWS)
        y = _causal_taps(z_ref, cw_ref, row0, CONV_ROWS, HALO_3, lanes)
        o_ref[rows, :] = (bg_ref[rows, :] * y).astype(o_ref.dtype)


def _mix_b(xn, w_in, conv_w, col0, seq_len, tn=256):
    m, d = xn.shape
    width = conv_w.shape[1]
    tm = TOKEN_TILE
    nj = width // tn
    c0 = col0 // tn
    prev, main = _halo_specs(tm, HALO_3, d)
    wspec = lambda part: pl.BlockSpec(
        (d, tn), functools.partial(lambda i, j, p: (0, c0 + p * nj + j), p=part))
    nbytes = (3 * (tm + HALO_3) * d * 2 + 6 * d * tn * 2
              + 4 * (tm + HALO_3) * tn * 4 + 2 * tm * tn * 2)
    return pl.pallas_call(
        functools.partial(_mix_b_kernel, tiles_per_seq=seq_len // tm),
        grid=(m // tm, nj),
        in_specs=[prev, main, wspec(0), wspec(1), wspec(2),
                  pl.BlockSpec((conv_w.shape[0], tn), lambda i, j: (0, j))],
        out_specs=pl.BlockSpec((tm, tn), lambda i, j: (i, j)),
        out_shape=jax.ShapeDtypeStruct((m, width), BF16),
        scratch_shapes=[pltpu.VMEM((tm + HALO_3, d), BF16),
                        pltpu.VMEM((tm, tn), F32),
                        pltpu.VMEM((tm + HALO_3, tn), F32)],
        compiler_params=_params(nbytes, 2),
        name="mix_b",
    )(xn, xn, w_in, w_in, w_in, conv_w)


def _q_attn_kernel(xn_ref, wq_ref, k_ref, v_ref, o_ref, *, scale):
    q = _dot(xn_ref[...], wq_ref[...]).astype(BF16)
    s = lax.dot_general(q, k_ref[...], (((1,), (1,)), ((), ())),
                        preferred_element_type=F32) * scale
    e = jnp.exp(s - jnp.max(s, axis=-1, keepdims=True))
    p = e / jnp.sum(e, axis=-1, keepdims=True)
    o_ref[...] = _dot(p.astype(BF16), v_ref[...]).astype(o_ref.dtype)


def _q_attn(xn, w_q, k, v, seq_len):
    m, d = xn.shape
    _, mem_len, _ = k.shape
    hd = d // N_XATTN_HEADS
    tm = TOKEN_TILE
    tiles_per_seq = seq_len // tm
    kv_spec = pl.BlockSpec((None, mem_len, hd), lambda i, h: (i // tiles_per_seq, 0, h))
    nbytes = (2 * tm * d * 2 + 2 * d * hd * 2 + 8 * mem_len * hd * 2 + 2 * tm * hd * 2
              + tm * hd * 10 + 3 * tm * mem_len * 4)
    return pl.pallas_call(
        functools.partial(_q_attn_kernel, scale=hd ** -0.5),
        grid=(m // tm, N_XATTN_HEADS),
        in_specs=[pl.BlockSpec((tm, d), lambda i, h: (i, 0)),
                  pl.BlockSpec((d, hd), lambda i, h: (0, h)),
                  kv_spec, kv_spec],
        out_specs=pl.BlockSpec((tm, hd), lambda i, h: (i, h)),
        out_shape=jax.ShapeDtypeStruct((m, d), BF16),
        compiler_params=_params(nbytes, 2),
        name="q_attn",
    )(xn, w_q, k, v)


def _ffn_up_kernel(halo_ref, main_ref, wg_ref, wu_ref, cw_ref,
                   o_ref, xs_ref, g_ref, up_ref, *, tiles_per_seq):
    tm, tn = o_ref.shape
    _stage_rows(halo_ref, main_ref, xs_ref, HALO_3)
    g_ref[...] = _dot(xs_ref[...], wg_ref[...])
    up_ref[...] = _dot(xs_ref[pl.ds(HALO_3, tm), :], wu_ref[...])

    @pl.when(pl.program_id(0) % tiles_per_seq == 0)
    def _():
        g_ref[pl.ds(0, HALO_3), :] = jnp.zeros((HALO_3, tn), F32)

    lanes = pl.ds(0, tn)
    for c in range(tm // CONV_ROWS):
        row0 = c * CONV_ROWS
        rows = pl.ds(row0, CONV_ROWS)
        y = _causal_taps(g_ref, cw_ref, row0, CONV_ROWS, HALO_3, lanes)
        o_ref[rows, :] = ((y * _sigmoid(y)) * up_ref[rows, :]).astype(o_ref.dtype)


def _ffn_up(xn, w_gate, w_up, conv_w, seq_len, tn=256):
    m, d = xn.shape
    d_ff = w_gate.shape[1]
    tm = TOKEN_TILE
    assert d_ff % tn == 0
    prev, main = _halo_specs(tm, HALO_3, d)
    wspec = pl.BlockSpec((d, tn), lambda i, j: (0, j))
    nbytes = (3 * (tm + HALO_3) * d * 2 + 4 * d * tn * 2
              + 4 * (tm + HALO_3) * tn * 4 + 2 * tm * tn * 2)
    return pl.pallas_call(
        functools.partial(_ffn_up_kernel, tiles_per_seq=seq_len // tm),
        grid=(m // tm, d_ff // tn),
        in_specs=[prev, main, wspec, wspec,
                  pl.BlockSpec((conv_w.shape[0], tn), lambda i, j: (0, j))],
        out_specs=pl.BlockSpec((tm, tn), lambda i, j: (i, j)),
        out_shape=jax.ShapeDtypeStruct((m, d_ff), BF16),
        scratch_shapes=[pltpu.VMEM((tm + HALO_3, d), BF16),
                        pltpu.VMEM((tm + HALO_3, tn), F32),
                        pltpu.VMEM((tm, tn), F32)],
        compiler_params=_params(nbytes, 2),
        name="ffn_up",
    )(xn, xn, w_gate, w_up, conv_w)


def kernel(x, mem, g_mix, w_in, conv_a_w, conv_a_b, ln_a_g, ln_a_b, conv_b_w, w_out,
           g_xattn, g_mem, w_q, w_k, w_v, w_o, g_ffn, w_gate, w_up, conv_f_w, w_down,
           g_final):
    b, t, d = x.shape
    mem_len = mem.shape[1]
    assert t % TOKEN_TILE == 0
    memn = _rmsnorm(mem.reshape(b * mem_len, d), g_mem, BF16)
    h = x.reshape(b * t, d)
    for l in range(g_mix.shape[0]):
        xn = _rmsnorm(h, g_mix[l], BF16)
        w_in_l = w_in[l].astype(BF16)
        conv_width = conv_a_w.shape[2]
        mix_a = _mix_a(xn, w_in_l, conv_a_w[l], conv_a_b[l], ln_a_g[l], ln_a_b[l], t)
        mix_b = _mix_b(xn, w_in_l, conv_b_w[l], 2 * conv_width, t)
        assert mix_a.shape == mix_b.shape
        h = _matmul([mix_a, mix_b], w_out[l].astype(BF16), h, F32, TOKEN_TILE, 512, "out_proj")

        xn = _rmsnorm(h, g_xattn[l], BF16)
        k = _matmul([memn], w_k[l].astype(BF16), None, BF16, b * mem_len, 512, "k_proj")
        v = _matmul([memn], w_v[l].astype(BF16), None, BF16, b * mem_len, 512, "v_proj")
        o = _q_attn(xn, w_q[l].astype(BF16), k.reshape(b, mem_len, d),
                    v.reshape(b, mem_len, d), t)
        h = _matmul([o], w_o[l].astype(BF16), h, F32, TOKEN_TILE, 512, "attn_out")

        xn = _rmsnorm(h, g_ffn[l], BF16)
        f = _ffn_up(xn, w_gate[l].astype(BF16), w_up[l].astype(BF16), conv_f_w[l], t)
        h = _matmul([f], w_down[l].astype(BF16), h, F32, 512, 512, "ffn_down")
    return _rmsnorm(h, g_final, F32).reshape(b, t, d)
```

```python
import functools

import jax
import jax.numpy as jnp
from jax import lax
from jax.experimental import pallas as pl
from jax.experimental.pallas import tpu as pltpu

GROUP_DIM = 128
N_XATTN_HEADS = 4
EPS = 1e-6

LANES = 128
V7X_F32_SUBLANES = 8
V7X_VMEM_BYTES = 64 * 1024 * 1024

TOKEN_TILE = 1024
NORM_ROWS = 256
CONV_ROWS = 64
F32 = jnp.float32
BF16 = jnp.bfloat16


def _vmem_limit(nbytes):
    return int(min(nbytes * 5 // 4 + (4 << 20), V7X_VMEM_BYTES - (6 << 20)))


def _params(nbytes, n_axes):
    return pltpu.CompilerParams(dimension_semantics=("arbitrary",) * n_axes,
                                vmem_limit_bytes=_vmem_limit(nbytes))


def _sigmoid(x):
    return 1.0 / (1.0 + jnp.exp(-x))


def _dot(a, b):
    return jnp.dot(a, b, preferred_element_type=F32)


def _wdot(a, w_ref):
    return _dot(a, w_ref[...].astype(BF16))


def _history_rows(n_taps):
    return -(-(n_taps - 1) // V7X_F32_SUBLANES) * V7X_F32_SUBLANES


def _rmsnorm_kernel(x_ref, g_ref, o_ref):
    x = x_ref[...]
    r = lax.rsqrt(jnp.mean(x * x, axis=-1, keepdims=True) + EPS)
    o_ref[...] = ((x * r) * g_ref[...]).astype(o_ref.dtype)


def _rmsnorm(x2d, g, out_dtype):
    n, d = x2d.shape
    rows = min(NORM_ROWS, n)
    return pl.pallas_call(
        _rmsnorm_kernel,
        grid=(n // rows,),
        in_specs=[pl.BlockSpec((rows, d), lambda i: (i, 0)),
                  pl.BlockSpec((1, d), lambda i: (0, 0))],
        out_specs=pl.BlockSpec((rows, d), lambda i: (i, 0)),
        out_shape=jax.ShapeDtypeStruct((n, d), out_dtype),
        compiler_params=_params(2 * rows * d * (4 + jnp.dtype(out_dtype).itemsize), 1),
        name="rmsnorm",
    )(x2d, g.reshape(1, d))


def _row_rsqrt(ssq, d):
    return lax.rsqrt(jnp.sum(ssq, axis=-1, keepdims=True) / d + EPS)


def _matmul_kernel(*refs, n_lhs, has_res, has_gain):
    a_refs = refs[:n_lhs]
    w_refs = refs[n_lhs:2 * n_lhs]
    extra = list(refs[2 * n_lhs:])
    acc = _wdot(a_refs[0][...], w_refs[0])
    for a_ref, w_ref in zip(a_refs[1:], w_refs[1:]):
        acc = acc + _wdot(a_ref[...], w_ref)
    if has_res:
        acc = acc + extra.pop(0)[...]
    if not has_gain:
        (o_ref,) = extra
        o_ref[...] = acc.astype(o_ref.dtype)
        return
    gain_ref, o_ref, scaled_ref, ssq_ref = extra
    o_ref[...] = acc
    scaled_ref[...] = (acc * gain_ref[...]).astype(scaled_ref.dtype)
    sq = acc * acc
    part = sq[:, :LANES]
    for c in range(1, acc.shape[1] // LANES):
        part = part + sq[:, c * LANES:(c + 1) * LANES]

    @pl.when(pl.program_id(1) == 0)
    def _():
        ssq_ref[...] = part

    @pl.when(pl.program_id(1) != 0)
    def _():
        ssq_ref[...] += part


def _matmul(lhs_list, w, res, out_dtype, tm, tn, name, next_gain=None):
    n_lhs = len(lhs_list)
    m, kp = lhs_list[0].shape
    k, n = w.shape
    assert kp * n_lhs == k and m % tm == 0 and n % tn == 0
    in_specs = [pl.BlockSpec((tm, kp), lambda i, j: (i, 0)) for _ in lhs_list]
    in_specs += [pl.BlockSpec((kp, tn), functools.partial(lambda i, j, p: (p, j), p=p))
                 for p in range(n_lhs)]
    args = list(lhs_list) + [w] * n_lhs
    out_size = jnp.dtype(out_dtype).itemsize
    w_size = w.dtype.itemsize
    nbytes = 2 * (tm * k * 2 + k * tn * w_size + tm * tn * out_size) + tm * tn * 4
    nbytes += k * tn * 2 if w_size != 2 else 0
    if res is not None:
        in_specs.append(pl.BlockSpec((tm, tn), lambda i, j: (i, j)))
        args.append(res)
        nbytes += 2 * tm * tn * 4
    tile = pl.BlockSpec((tm, tn), lambda i, j: (i, j))
    out_specs = tile
    out_shape = jax.ShapeDtypeStruct((m, n), out_dtype)
    if next_gain is not None:
        assert out_dtype == F32
        in_specs.append(pl.BlockSpec((1, tn), lambda i, j: (0, j)))
        args.append(next_gain.reshape(1, n))
        out_specs = [tile, tile, pl.BlockSpec((tm, LANES), lambda i, j: (i, 0))]
        out_shape = [out_shape, jax.ShapeDtypeStruct((m, n), BF16),
                     jax.ShapeDtypeStruct((m, LANES), F32)]
        nbytes += 2 * tm * tn * 2 + 2 * tm * LANES * 4 + 2 * tm * tn * 4
    return pl.pallas_call(
        functools.partial(_matmul_kernel, n_lhs=n_lhs, has_res=res is not None,
                          has_gain=next_gain is not None),
        grid=(m // tm, n // tn),
        in_specs=in_specs,
        out_specs=out_specs,
        out_shape=out_shape,
        compiler_params=_params(nbytes, 2),
        name=name,
    )(*args)


class _LaggedGrid:
    def __init__(self, n_rows, n_cols, tiles_per_seq):
        self.n_rows = n_rows
        self.n_pairs = n_rows * n_cols
        self.n_steps = self.n_pairs + 1
        self.tiles_per_seq = tiles_per_seq

    def _pair_in(self, s):
        return jnp.minimum(s, self.n_pairs - 1)

    def _pair_out(self, s):
        return jnp.maximum(s - 1, 0)

    def new_col(self, s):
        return self._pair_in(s) % self.n_rows == 0

    def seq_start_out(self, s):
        return (self._pair_out(s) % self.n_rows) % self.tiles_per_seq == 0

    def row_in_spec(self, tm, d):
        return pl.BlockSpec((tm, d), lambda s: (self._pair_in(s) % self.n_rows, 0))

    def col_in_spec(self, rows, tn, first=0):
        return pl.BlockSpec((rows, tn), lambda s: (0, first + self._pair_in(s) // self.n_rows))

    def col_out_spec(self, rows, tn):
        return pl.BlockSpec((rows, tn), lambda s: (0, self._pair_out(s) // self.n_rows))

    def out_spec(self, tm, tn):
        return pl.BlockSpec(
            (tm, tn), lambda s: (self._pair_out(s) % self.n_rows, self._pair_out(s) // self.n_rows))


def _round_weights(new_col, *pairs):
    @pl.when(new_col)
    def _():
        for w_ref, wb_ref in pairs:
            wb_ref[...] = w_ref[...].astype(BF16)


def _ping_pong(step, body, bufs_a, bufs_b, carries):
    @pl.when(step == 0)
    def _():
        for ref in tuple(bufs_b) + tuple(carries):
            ref[...] = jnp.zeros(ref.shape, ref.dtype)

    @pl.when(step % 2 == 0)
    def _():
        body(bufs_a, bufs_b)

    @pl.when(step % 2 == 1)
    def _():
        body(bufs_b, bufs_a)


def _causal_taps(src_ref, carry_ref, w_ref, row0, rows, lanes, seq_start):
    n_taps = w_ref.shape[0]
    back = carry_ref.shape[0]
    assert back == _history_rows(n_taps)
    if row0 == 0:
        history = jnp.where(seq_start, 0.0, carry_ref[:, lanes])
        ext = jnp.concatenate([history, src_ref[pl.ds(0, rows), lanes]], axis=0)
    else:
        ext = src_ref[pl.ds(row0 - back, rows + back), lanes]
    acc = None
    for b in range(V7X_F32_SUBLANES):
        shifted = ext if b == 0 else pltpu.roll(ext, b, axis=0)
        for a in range(back // V7X_F32_SUBLANES + 1):
            k = n_taps - 1 - (V7X_F32_SUBLANES * a + b)
            if k < 0:
                continue
            first = back - V7X_F32_SUBLANES * a
            term = shifted[first:first + rows] * w_ref[pl.ds(k, 1), lanes]
            acc = term if acc is None else acc + term
    return acc


def _save_history(src_ref, carry_ref):
    back = carry_ref.shape[0]
    carry_ref[...] = src_ref[pl.ds(src_ref.shape[0] - back, back), :]


def _mix_a_kernel(x_ref, wv_ref, wg_ref, cw_ref, cb_ref, lg_ref, lb_ref, o_ref,
                  wvb_ref, wgb_ref, u_ref, carry_ref, val_a, gate_a, val_b, gate_b, *, lag):
    tm, tn = o_ref.shape
    step = pl.program_id(0)
    _round_weights(lag.new_col(step), (wv_ref, wvb_ref), (wg_ref, wgb_ref))
    seq_start = lag.seq_start_out(step)

    def body(fill, drain):
        val_ref, gate_ref = drain
        for c in range(tm // CONV_ROWS):
            row0 = c * CONV_ROWS
            rows = pl.ds(row0, CONV_ROWS)
            u_ref[rows, :] = val_ref[rows, :] * _sigmoid(gate_ref[rows, :])
            for g in range(tn // GROUP_DIM):
                lanes = pl.ds(g * GROUP_DIM, GROUP_DIM)
                y = _causal_taps(u_ref, carry_ref, cw_ref, row0, CONV_ROWS, lanes, seq_start)
                y = y + cb_ref[:, lanes]
                mu = jnp.mean(y, axis=-1, keepdims=True)
                yc = y - mu
                var = jnp.mean(yc * yc, axis=-1, keepdims=True)
                yn = (yc * lax.rsqrt(var + EPS)) * lg_ref[:, lanes] + lb_ref[:, lanes]
                o_ref[rows, lanes] = (yn * _sigmoid(yn)).astype(o_ref.dtype)
        _save_history(u_ref, carry_ref)

        val_ref, gate_ref = fill
        val_ref[...] = _dot(x_ref[...], wvb_ref[...])
        gate_ref[...] = _dot(x_ref[...], wgb_ref[...])

    _ping_pong(step, body, (val_a, gate_a), (val_b, gate_b), (carry_ref,))


def _mix_a(xn, w_in, conv_w, conv_b, ln_g, ln_b, seq_len, tn=256):
    m, d = xn.shape
    width = conv_w.shape[1]
    tm = TOKEN_TILE
    nj = width // tn
    lag = _LaggedGrid(m // tm, nj, seq_len // tm)
    product = pltpu.VMEM((tm, tn), F32)
    rounded = pltpu.VMEM((d, tn), BF16)
    nbytes = (2 * tm * d * 2 + 4 * d * tn * w_in.dtype.itemsize + 2 * d * tn * 2
              + 5 * tm * tn * 4 + 2 * tm * tn * 2)
    return pl.pallas_call(
        functools.partial(_mix_a_kernel, lag=lag),
        grid=(lag.n_steps,),
        in_specs=[lag.row_in_spec(tm, d),
                  lag.col_in_spec(d, tn),
                  lag.col_in_spec(d, tn, first=nj),
                  lag.col_out_spec(conv_w.shape[0], tn), lag.col_out_spec(1, tn),
                  lag.col_out_spec(1, tn), lag.col_out_spec(1, tn)],
        out_specs=lag.out_spec(tm, tn),
        out_shape=jax.ShapeDtypeStruct((m, width), BF16),
        scratch_shapes=[rounded, rounded, product,
                        pltpu.VMEM((_history_rows(conv_w.shape[0]), tn), F32),
                        product, product, product, product],
        compiler_params=_params(nbytes, 1),
        name="mix_a",
    )(xn, w_in, w_in, conv_w, conv_b.reshape(1, -1), ln_g.reshape(1, -1),
      ln_b.reshape(1, -1))


def _mix_b_kernel(x_ref, wb_ref, wc_ref, wh_ref, cw_ref, o_ref,
                  wbb_ref, wcb_ref, whb_ref, carry_ref, bg_a, z_a, bg_b, z_b, *, lag):
    tm, tn = o_ref.shape
    step = pl.program_id(0)
    _round_weights(lag.new_col(step), (wb_ref, wbb_ref), (wc_ref, wcb_ref), (wh_ref, whb_ref))
    seq_start = lag.seq_start_out(step)

    def body(fill, drain):
        bg_ref, z_ref = fill
        bg_ref[...] = _dot(x_ref[...], wbb_ref[...])
        z_ref[...] = _dot(x_ref[...], wcb_ref[...]) * _dot(x_ref[...], whb_ref[...])

        bg_ref, z_ref = drain
        lanes = pl.ds(0, tn)
        for c in range(tm // CONV_ROWS):
            row0 = c * CONV_ROWS
            rows = pl.ds(row0, CONV_ROWS)
            y = _causal_taps(z_ref, carry_ref, cw_ref, row0, CONV_ROWS, lanes, seq_start)
            o_ref[rows, :] = (bg_ref[rows, :] * y).astype(o_ref.dtype)
        _save_history(z_ref, carry_ref)

    _ping_pong(step, body, (bg_a, z_a), (bg_b, z_b), (carry_ref,))


def _mix_b(xn, w_in, conv_w, col0, seq_len, tn=256):
    m, d = xn.shape
    width = conv_w.shape[1]
    tm = TOKEN_TILE
    nj = width // tn
    lag = _LaggedGrid(m // tm, nj, seq_len // tm)
    wspec = lambda part: lag.col_in_spec(d, tn, first=col0 // tn + part * nj)
    product = pltpu.VMEM((tm, tn), F32)
    rounded = pltpu.VMEM((d, tn), BF16)
    nbytes = (2 * tm * d * 2 + 6 * d * tn * w_in.dtype.itemsize + 3 * d * tn * 2
              + 4 * tm * tn * 4 + 2 * tm * tn * 2)
    return pl.pallas_call(
        functools.partial(_mix_b_kernel, lag=lag),
        grid=(lag.n_steps,),
        in_specs=[lag.row_in_spec(tm, d), wspec(0), wspec(1), wspec(2),
                  lag.col_out_spec(conv_w.shape[0], tn)],
        out_specs=lag.out_spec(tm, tn),
        out_shape=jax.ShapeDtypeStruct((m, width), BF16),
        scratch_shapes=[rounded, rounded, rounded,
                        pltpu.VMEM((_history_rows(conv_w.shape[0]), tn), F32),
                        product, product, product, product],
        compiler_params=_params(nbytes, 1),
        name="mix_b",
    )(xn, w_in, w_in, w_in, conv_w)


def _q_attn_kernel(xg_ref, ssq_ref, wq_ref, k_ref, v_ref, o_ref, *, scale):
    r = _row_rsqrt(ssq_ref[...], xg_ref.shape[1])
    q = _dot(xg_ref[...], wq_ref[...]).astype(BF16)
    s = lax.dot_general(q, k_ref[...], (((1,), (1,)), ((), ())),
                        preferred_element_type=F32) * (r * scale)
    e = jnp.exp(s - jnp.max(s, axis=-1, keepdims=True))
    p = e / jnp.sum(e, axis=-1, keepdims=True)
    o_ref[...] = _dot(p.astype(BF16), v_ref[...]).astype(o_ref.dtype)


def _q_attn(xn, ssq, w_q, k, v, seq_len):
    m, d = xn.shape
    _, mem_len, _ = k.shape
    hd = d // N_XATTN_HEADS
    tm = TOKEN_TILE
    tiles_per_seq = seq_len // tm
    kv_spec = pl.BlockSpec((None, mem_len, hd), lambda i, h: (i // tiles_per_seq, 0, h))
    nbytes = (2 * tm * d * 2 + 2 * d * hd * 2 + 8 * mem_len * hd * 2 + 2 * tm * hd * 2
              + tm * hd * 10 + 3 * tm * mem_len * 4)
    return pl.pallas_call(
        functools.partial(_q_attn_kernel, scale=hd ** -0.5),
        grid=(m // tm, N_XATTN_HEADS),
        in_specs=[pl.BlockSpec((tm, d), lambda i, h: (i, 0)),
                  pl.BlockSpec((tm, LANES), lambda i, h: (i, 0)),
                  pl.BlockSpec((d, hd), lambda i, h: (0, h)),
                  kv_spec, kv_spec],
        out_specs=pl.BlockSpec((tm, hd), lambda i, h: (i, h)),
        out_shape=jax.ShapeDtypeStruct((m, d), BF16),
        compiler_params=_params(nbytes, 2),
        name="q_attn",
    )(xn, ssq, w_q, k, v)


def _ffn_up_kernel(x_ref, ssq_ref, wg_ref, wu_ref, cw_ref, o_ref,
                   wgb_ref, wub_ref, carry_ref, g_a, up_a, g_b, up_b, *, lag):
    tm, tn = o_ref.shape
    step = pl.program_id(0)
    _round_weights(lag.new_col(step), (wg_ref, wgb_ref), (wu_ref, wub_ref))
    seq_start = lag.seq_start_out(step)

    def body(fill, drain):
        g_ref, up_ref = fill
        r = _row_rsqrt(ssq_ref[...], x_ref.shape[1])
        g_ref[...] = _dot(x_ref[...], wgb_ref[...]) * r
        up_ref[...] = _dot(x_ref[...], wub_ref[...]) * r

        g_ref, up_ref = drain
        lanes = pl.ds(0, tn)
        for c in range(tm // CONV_ROWS):
            row0 = c * CONV_ROWS
            rows = pl.ds(row0, CONV_ROWS)
            y = _causal_taps(g_ref, carry_ref, cw_ref, row0, CONV_ROWS, lanes, seq_start)
            o_ref[rows, :] = ((y * _sigmoid(y)) * up_ref[rows, :]).astype(o_ref.dtype)
        _save_history(g_ref, carry_ref)

    _ping_pong(step, body, (g_a, up_a), (g_b, up_b), (carry_ref,))


def _ffn_up(xn, ssq, w_gate, w_up, conv_w, seq_len, tn=256):
    m, d = xn.shape
    d_ff = w_gate.shape[1]
    tm = TOKEN_TILE
    assert d_ff % tn == 0
    lag = _LaggedGrid(m // tm, d_ff // tn, seq_len // tm)
    product = pltpu.VMEM((tm, tn), F32)
    rounded = pltpu.VMEM((d, tn), BF16)
    nbytes = (2 * tm * d * 2 + 4 * d * tn * w_gate.dtype.itemsize + 2 * d * tn * 2
              + 4 * tm * tn * 4 + 2 * tm * tn * 2 + 2 * tm * LANES * 4)
    return pl.pallas_call(
        functools.partial(_ffn_up_kernel, lag=lag),
        grid=(lag.n_steps,),
        in_specs=[lag.row_in_spec(tm, d), lag.row_in_spec(tm, LANES),
                  lag.col_in_spec(d, tn), lag.col_in_spec(d, tn),
                  lag.col_out_spec(conv_w.shape[0], tn)],
        out_specs=lag.out_spec(tm, tn),
        out_shape=jax.ShapeDtypeStruct((m, d_ff), BF16),
        scratch_shapes=[rounded, rounded,
                        pltpu.VMEM((_history_rows(conv_w.shape[0]), tn), F32),
                        product, product, product, product],
        compiler_params=_params(nbytes, 1),
        name="ffn_up",
    )(xn, ssq, w_gate, w_up, conv_w)


def kernel(x, mem, g_mix, w_in, conv_a_w, conv_a_b, ln_a_g, ln_a_b, conv_b_w, w_out,
           g_xattn, g_mem, w_q, w_k, w_v, w_o, g_ffn, w_gate, w_up, conv_f_w, w_down,
           g_final):
    b, t, d = x.shape
    mem_len = mem.shape[1]
    assert t % TOKEN_TILE == 0
    memn = _rmsnorm(mem.reshape(b * mem_len, d), g_mem, BF16)
    h = x.reshape(b * t, d)
    for l in range(g_mix.shape[0]):
        xn = _rmsnorm(h, g_mix[l], BF16)
        conv_width = conv_a_w.shape[2]
        mix_a = _mix_a(xn, w_in[l], conv_a_w[l], conv_a_b[l], ln_a_g[l], ln_a_b[l], t)
        mix_b = _mix_b(xn, w_in[l], conv_b_w[l], 2 * conv_width, t)
        assert mix_a.shape == mix_b.shape
        h, hg, ssq = _matmul([mix_a, mix_b], w_out[l].astype(BF16), h, F32, TOKEN_TILE, 512,
                             "out_proj", next_gain=g_xattn[l])

        k = _matmul([memn], w_k[l], None, BF16, b * mem_len, 512, "k_proj")
        v = _matmul([memn], w_v[l], None, BF16, b * mem_len, 512, "v_proj")
        o = _q_attn(hg, ssq, w_q[l].astype(BF16), k.reshape(b, mem_len, d),
                    v.reshape(b, mem_len, d), t)
        h, hg, ssq = _matmul([o], w_o[l].astype(BF16), h, F32, TOKEN_TILE, 512, "attn_out",
                             next_gain=g_ffn[l])

        f = _ffn_up(hg, ssq, w_gate[l], w_up[l], conv_f_w[l], t)
        h = _matmul([f], w_down[l].astype(BF16), h, F32, 512, 512, "ffn_down")
    return _rmsnorm(h, g_final, F32).reshape(b, t, d)
```
